```python
import jax
import jax.numpy as jnp
from jax import lax
import numpy as np

D_MODEL = 4096
BATCH = 1
SEQ = 16384
DEPTH = 4

GRID_W = 64
CTX_LEN = 256
N_MIXERS = 3
HEAD_DIM = 128
N_HEADS = D_MODEL // HEAD_DIM
NA_WIN_ROWS = 8
NA_WIN_COLS = 16
SWA_KV_HEADS = 8
SWA_WINDOW = 128
SWA_BLOCK = 128
HG_EXPAND = 128
HG_HEADS = D_MODEL // HG_EXPAND
HG_DK = D_MODEL // HG_HEADS
HG_DV = D_MODEL // HG_HEADS
HG_CHUNK = 64
FFN_DIM = 11008
N_EXPERTS = 8
TOP_K = 2
EXPERT_DIM = 1536
ADA_RANK = 256
ROPE_THETA = 10000.0
NORM_EPS = 1e-6
NEG_INF = -1e30

kernel_name = "hybrid_latent_dit_trunk"


def rmsnorm(x, g):
    xf = x.astype(jnp.float32)
    y = xf * lax.rsqrt(jnp.mean(xf * xf, axis=-1, keepdims=True) + NORM_EPS)
    return (y * g.astype(jnp.float32)).astype(x.dtype)


def adaln(cvec, down, up, bias):
    m = jax.nn.silu(cvec) @ down @ up + bias
    return jnp.split(m, 6, axis=-1)


def modulate(h, shift, scale):
    return h * (1 + scale) + shift


def rope_1d(x, pos):
    half = x.shape[-1] // 2
    inv = ROPE_THETA ** (-jnp.arange(half, dtype=jnp.float32) / half)
    ang = pos.astype(jnp.float32)[:, None] * inv[None, :]
    cos = jnp.cos(ang)[None, :, None, :].astype(x.dtype)
    sin = jnp.sin(ang)[None, :, None, :].astype(x.dtype)
    a, b = x[..., :half], x[..., half:]
    return jnp.concatenate([a * cos - b * sin, b * cos + a * sin], axis=-1)


def rope_2d(x):
    t = jnp.arange(x.shape[1])
    rd = x.shape[-1] // 2
    return jnp.concatenate([rope_1d(x[..., :rd], t // GRID_W), rope_1d(x[..., rd:], t % GRID_W)], axis=-1)


def split_heads(a, n_heads):
    B, T, _ = a.shape
    return a.reshape(B, T, n_heads, -1)


def context_attention(qc, kc, vc, sinks=None):
    B, L, H, hd = qc.shape
    kvh = kc.shape[2]
    qg = qc.reshape(B, L, kvh, H // kvh, hd)
    s = jnp.einsum('bqkgd,bskd->bkgqs', qg, kc).astype(jnp.float32) * hd ** -0.5
    if sinks is not None:
        sink = jnp.broadcast_to(sinks.astype(jnp.float32).reshape(kvh, H // kvh)[None, :, :, None, None],
                                s.shape[:-1] + (1,))
        s = jnp.concatenate([s, sink], axis=-1)
    p = jax.nn.softmax(s, axis=-1)[..., :L].astype(vc.dtype)
    o = jnp.einsum('bkgqs,bskd->bqkgd', p, vc)
    return o.reshape(B, L, H, hd)


def neighbourhood_attention(q, k, v, kc, vc, rpb):
    B, N, H, hd = q.shape
    rows = N // GRID_W
    kr = min(NA_WIN_ROWS, rows)
    kw = NA_WIN_COLS
    scale = hd ** -0.5
    qg = q.reshape(B, rows, GRID_W, H, hd)
    kg = k.reshape(B, rows, GRID_W, H, hd)
    vg = v.reshape(B, rows, GRID_W, H, hd)
    cols = jnp.arange(GRID_W)
    col_idx = jnp.clip(cols - kw // 2, 0, GRID_W - kw)[:, None] + jnp.arange(kw)[None, :]
    col_off = col_idx - cols[:, None] + (NA_WIN_COLS - 1)

    def one_row(r):
        rs = jnp.clip(r - kr // 2, 0, rows - kr)
        q_r = lax.dynamic_index_in_dim(qg, r, axis=1, keepdims=False)
        k_band = lax.dynamic_slice_in_dim(kg, rs, kr, axis=1)
        v_band = lax.dynamic_slice_in_dim(vg, rs, kr, axis=1)
        k_win = k_band[:, :, col_idx]
        v_win = v_band[:, :, col_idx]
        row_off = rs + jnp.arange(kr) - r + (NA_WIN_ROWS - 1)
        bias = rpb[:, row_off[:, None, None], col_off[None, :, :]]
        s_loc = (jnp.einsum('bqhd,biqjhd->bhqij', q_r, k_win).astype(jnp.float32) * scale
                 + jnp.transpose(bias, (0, 2, 1, 3))[None].astype(jnp.float32))
        s_ctx = jnp.einsum('bqhd,bkhd->bhqk', q_r, kc).astype(jnp.float32) * scale
        s = jnp.concatenate([s_loc.reshape(B, H, GRID_W, kr * kw), s_ctx], axis=-1)
        p = jax.nn.softmax(s, axis=-1).astype(v.dtype)
        p_loc = p[..., :kr * kw].reshape(B, H, GRID_W, kr, kw)
        p_ctx = p[..., kr * kw:]
        return (jnp.einsum('bhqij,biqjhd->bqhd', p_loc, v_win)
                + jnp.einsum('bhqk,bkhd->bqhd', p_ctx, vc))

    out = lax.map(one_row, jnp.arange(rows))
    return jnp.moveaxis(out, 0, 1).reshape(B, N, H, hd)


def window_attention(q, k, v, kc, vc, sinks):
    B, N, H, hd = q.shape
    kvh = k.shape[2]
    g = H // kvh
    nb = N // SWA_BLOCK
    span = SWA_BLOCK + 2 * SWA_WINDOW
    scale = hd ** -0.5
    pad = ((0, 0), (SWA_WINDOW, SWA_WINDOW), (0, 0), (0, 0))
    kp = jnp.pad(k, pad)
    vp = jnp.pad(v, pad)
    qb = q.reshape(B, nb, SWA_BLOCK, kvh, g, hd)
    sink_col = sinks.astype(jnp.float32).reshape(kvh, g)[None, :, :, None, None]
    off = jnp.arange(span) - SWA_WINDOW
    qi = jnp.arange(SWA_BLOCK)

    def one_block(b):
        start = b * SWA_BLOCK
        q_b = lax.dynamic_index_in_dim(qb, b, axis=1, keepdims=False)
        k_b = lax.dynamic_slice_in_dim(kp, start, span, axis=1)
        v_b = lax.dynamic_slice_in_dim(vp, start, span, axis=1)
        kpos = start + off
        valid = ((jnp.abs(qi[:, None] - off[None, :]) <= SWA_WINDOW)
                 & (kpos >= 0)[None, :] & (kpos < N)[None, :])
        s_loc = jnp.einsum('bqkgd,bskd->bkgqs', q_b, k_b).astype(jnp.float32) * scale
        s_loc = jnp.where(valid, s_loc, NEG_INF)
        s_ctx = jnp.einsum('bqkgd,bskd->bkgqs', q_b, kc).astype(jnp.float32) * scale
        sink = jnp.broadcast_to(sink_col, s_loc.shape[:-1] + (1,))
        p = jax.nn.softmax(jnp.concatenate([s_loc, s_ctx, sink], axis=-1), axis=-1).astype(v.dtype)
        p_loc = p[..., :span]
        p_ctx = p[..., span:span + kc.shape[1]]
        o = (jnp.einsum('bkgqs,bskd->bqkgd', p_loc, v_b)
             + jnp.einsum('bkgqs,bskd->bqkgd', p_ctx, vc))
        return o.reshape(B, SWA_BLOCK, H, hd)

    out = lax.map(one_block, jnp.arange(nb))
    return jnp.moveaxis(out, 0, 1).reshape(B, N, H, hd)


def gla_chunk_scan(q, k, v, log_f, s0):
    B, T, H, dk = q.shape
    C = HG_CHUNK
    n = T // C

    def chunks(a):
        return jnp.moveaxis(a.reshape(B, n, C, H, a.shape[-1]), 1, 0)

    causal = jnp.tril(jnp.ones((C, C), dtype=bool))

    def step(S, inp):
        qc, kc, vc, gc = inp
        b = jnp.cumsum(gc, axis=1)
        b_last = b[:, -1]
        qe = qc * jnp.exp(b)
        ke = kc * jnp.exp(-b)
        a = jnp.where(causal, jnp.einsum('bthd,bshd->bhts', qe, ke), 0.0)
        o = jnp.einsum('bthd,bhdv->bthv', qe, S) + jnp.einsum('bhts,bshv->bthv', a, vc)
        kd = kc * jnp.exp(b_last[:, None] - b)
        S = jnp.exp(b_last)[..., None] * S + jnp.einsum('bshd,bshv->bhdv', kd, vc)
        return S, o

    S, o = lax.scan(step, s0, (chunks(q), chunks(k), chunks(v), chunks(log_f)))
    return jnp.moveaxis(o, 0, 1).reshape(B, T, H, v.shape[-1]), S


def neighbourhood_mixer(hc, hl, w_in, rpb, w_out, need_ctx):
    D = D_MODEL
    ql, kl, vl = [split_heads(a, N_HEADS) for a in jnp.split(hl @ w_in, 3, axis=-1)]
    if need_ctx:
        qc, kc, vc = [split_heads(a, N_HEADS) for a in jnp.split(hc @ w_in, 3, axis=-1)]
    else:
        kc, vc = [split_heads(a, N_HEADS) for a in jnp.split(hc @ w_in[:, D:], 2, axis=-1)]
    B, N = hl.shape[:2]
    out_l = neighbourhood_attention(ql, kl, vl, kc, vc, rpb).reshape(B, N, D) @ w_out
    out_c = context_attention(qc, kc, vc).reshape(hc.shape) @ w_out if need_ctx else None
    return out_l, out_c


def window_mixer(hc, hl, w_in, sinks, w_out, need_ctx):
    D = D_MODEL
    kvd = SWA_KV_HEADS * HEAD_DIM
    zl = hl @ w_in
    ql = rope_2d(split_heads(zl[..., :D], N_HEADS))
    kl = rope_2d(split_heads(zl[..., D:D + kvd], SWA_KV_HEADS))
    vl = split_heads(zl[..., D + kvd:], SWA_KV_HEADS)
    zc = hc @ w_in if need_ctx else hc @ w_in[:, D:]
    zc_kv = zc[..., D:] if need_ctx else zc
    kc = split_heads(zc_kv[..., :kvd], SWA_KV_HEADS)
    vc = split_heads(zc_kv[..., kvd:], SWA_KV_HEADS)
    B, N = hl.shape[:2]
    out_l = window_attention(ql, kl, vl, kc, vc, sinks).reshape(B, N, D) @ w_out
    if need_ctx:
        qc = split_heads(zc[..., :D], N_HEADS)
        out_c = context_attention(qc, kc, vc, sinks).reshape(hc.shape) @ w_out
    else:
        out_c = None
    return out_l, out_c


def hgrn2_mixer(hc, hl, w_in, lb_fwd, lb_bwd, g_norm, w_out, need_ctx):
    f32 = jnp.float32

    def gates(h):
        B, T, _ = h.shape
        q, i, g, ff, fb = jnp.split(h @ w_in, 5, axis=-1)
        heads = lambda a: a.astype(f32).reshape(B, T, HG_HEADS, HG_DK)
        f_fwd = lb_fwd + (1 - lb_fwd) * jax.nn.sigmoid(ff.astype(f32))
        f_bwd = lb_bwd + (1 - lb_bwd) * jax.nn.sigmoid(fb.astype(f32))
        return jax.nn.silu(heads(q)), heads(i), g, heads(f_fwd), heads(f_bwd)

    def flip(a):
        return jnp.flip(a, axis=1)

    def bidir(q, i, f_fwd, f_bwd, s_fwd, s_bwd):
        o_f, s_f = gla_chunk_scan(q, 1 - f_fwd, i, jnp.log(f_fwd), s_fwd)
        o_b, s_b = gla_chunk_scan(flip(q), flip(1 - f_bwd), flip(i), flip(jnp.log(f_bwd)), s_bwd)
        return o_f + flip(o_b), s_f, s_b

    def readout(o, g):
        B, T = g.shape[:2]
        o = rmsnorm(o, g_norm) * jax.nn.silu(g.astype(f32).reshape(B, T, HG_HEADS, HG_DV))
        return o.reshape(B, T, D_MODEL).astype(g.dtype) @ w_out

    B = hc.shape[0]
    zeros = jnp.zeros((B, HG_HEADS, HG_DK, HG_DV), f32)
    qc, ic, gc, fcf, fcb = gates(hc)
    oc, sc_f, sc_b = bidir(qc, ic, fcf, fcb, zeros, zeros)
    ql, il, gl, flf, flb = gates(hl)
    ol, _, _ = bidir(ql, il, flf, flb, sc_f, sc_b)
    out_l = readout(ol, gl)
    out_c = readout(oc, gc) if need_ctx else None
    return out_l, out_c


def swiglu(h, w_gu, w_down):
    a, b = jnp.split(h @ w_gu, 2, axis=-1)
    return (jax.nn.silu(a) * b) @ w_down


def moe_swiglu(h, router, w_gu, w_down):
    logits = (h @ router).astype(jnp.float32)
    top_v, top_i = lax.top_k(logits, TOP_K)
    weights = jax.nn.softmax(top_v, axis=-1)
    combine = jnp.sum(jax.nn.one_hot(top_i, N_EXPERTS, dtype=jnp.float32) * weights[..., None], axis=-2)
    combine = combine.astype(h.dtype)
    out = jnp.zeros_like(h)
    for e in range(N_EXPERTS):
        out = out + combine[..., e:e + 1] * swiglu(h, w_gu[e], w_down[e])
    return out


def setup_inputs(seed: int = 0) -> dict:
    key = jax.random.key(seed)
    ks = list(jax.random.split(key, 64))
    D = D_MODEL

    def normal(shape, scale):
        return jax.random.normal(ks.pop(), shape, jnp.float32) * scale

    def gain(n):
        return 1.0 + normal((n,), 0.1)

    inp = {}
    inp["x"] = normal((BATCH, SEQ, D), 1.0)
    inp["c"] = normal((BATCH, D), 1.0)
    inp["ctx"] = normal((BATCH, CTX_LEN, D), 1.0)
    inp["c_ctx"] = normal((D,), 1.0)
    for l in range(DEPTH):
        inp[f"norm_mix_{l}"] = gain(D)
        inp[f"norm_ffn_{l}"] = gain(D)
        inp[f"ada_down_{l}"] = normal((D, ADA_RANK), D ** -0.5)
        inp[f"ada_up_{l}"] = normal((ADA_RANK, 6 * D), 0.5 * ADA_RANK ** -0.5)
        inp[f"ada_bias_{l}"] = normal((6 * D,), 0.02)
        kind = l % N_MIXERS
        if kind == 0:
            inp[f"na_w_in_{l}"] = normal((D, 3 * D), D ** -0.5)
            inp[f"na_rpb_{l}"] = normal((N_HEADS, 2 * NA_WIN_ROWS - 1, 2 * NA_WIN_COLS - 1), 0.1)
            inp[f"na_w_out_{l}"] = normal((D, D), D ** -0.5)
        elif kind == 1:
            inp[f"swa_w_in_{l}"] = normal((D, D + 2 * SWA_KV_HEADS * HEAD_DIM), D ** -0.5)
            inp[f"swa_sinks_{l}"] = normal((N_HEADS,), 0.5)
            inp[f"swa_w_out_{l}"] = normal((D, D), D ** -0.5)
        else:
            inp[f"hg_w_in_{l}"] = normal((D, 5 * D), D ** -0.5)
            inp[f"hg_gnorm_{l}"] = gain(HG_DV)
            inp[f"hg_w_out_{l}"] = normal((D, D), D ** -0.5)
        if l % 2 == 1:
            inp[f"moe_router_{l}"] = normal((D, N_EXPERTS), D ** -0.5)
            inp[f"moe_w_gu_{l}"] = normal((N_EXPERTS, D, 2 * EXPERT_DIM), D ** -0.5)
            inp[f"moe_w_down_{l}"] = normal((N_EXPERTS, EXPERT_DIM, D), EXPERT_DIM ** -0.5)
        else:
            inp[f"ffn_w_gu_{l}"] = normal((D, 2 * FFN_DIM), D ** -0.5)
            inp[f"ffn_w_down_{l}"] = normal((FFN_DIM, D), FFN_DIM ** -0.5)
    inp["hgrn_lower_bounds"] = normal((2, DEPTH, D), 0.1)
    inp["final_norm"] = gain(D)
    return inp


def reference(x, c, ctx, c_ctx,
              norm_mix_0, norm_ffn_0, ada_down_0, ada_up_0, ada_bias_0,
              na_w_in_0, na_rpb_0, na_w_out_0, ffn_w_gu_0, ffn_w_down_0,
              norm_mix_1, norm_ffn_1, ada_down_1, ada_up_1, ada_bias_1,
              swa_w_in_1, swa_sinks_1, swa_w_out_1, moe_router_1, moe_w_gu_1, moe_w_down_1,
              norm_mix_2, norm_ffn_2, ada_down_2, ada_up_2, ada_bias_2,
              hg_w_in_2, hg_gnorm_2, hg_w_out_2, ffn_w_gu_2, ffn_w_down_2,
              norm_mix_3, norm_ffn_3, ada_down_3, ada_up_3, ada_bias_3,
              na_w_in_3, na_rpb_3, na_w_out_3, moe_router_3, moe_w_gu_3, moe_w_down_3,
              hgrn_lower_bounds, final_norm):
    layers = [
        ((norm_mix_0, norm_ffn_0, ada_down_0, ada_up_0, ada_bias_0),
         (na_w_in_0, na_rpb_0, na_w_out_0), (ffn_w_gu_0, ffn_w_down_0)),
        ((norm_mix_1, norm_ffn_1, ada_down_1, ada_up_1, ada_bias_1),
         (swa_w_in_1, swa_sinks_1, swa_w_out_1), (moe_router_1, moe_w_gu_1, moe_w_down_1)),
        ((norm_mix_2, norm_ffn_2, ada_down_2, ada_up_2, ada_bias_2),
         (hg_w_in_2, hg_gnorm_2, hg_w_out_2), (ffn_w_gu_2, ffn_w_down_2)),
        ((norm_mix_3, norm_ffn_3, ada_down_3, ada_up_3, ada_bias_3),
         (na_w_in_3, na_rpb_3, na_w_out_3), (moe_router_3, moe_w_gu_3, moe_w_down_3)),
    ]
    lb = jax.nn.softmax(hgrn_lower_bounds.astype(jnp.float32), axis=1)
    lb = jnp.cumsum(lb, axis=1) - lb[:, :1]

    xc, xl = ctx, x
    for l in range(DEPTH):
        (nm, nf, ad, au, ab), mix_p, ffn_p = layers[l]
        need_ctx = l < DEPTH - 1
        mod_l = adaln(c[:, None, :], ad, au, ab)
        mod_c = adaln(c_ctx, ad, au, ab)
        hl = modulate(rmsnorm(xl, nm), mod_l[0], mod_l[1])
        hc = modulate(rmsnorm(xc, nm), mod_c[0], mod_c[1])
        kind = l % N_MIXERS
        if kind == 0:
            ol, oc = neighbourhood_mixer(hc, hl, mix_p[0], mix_p[1], mix_p[2], need_ctx)
        elif kind == 1:
            ol, oc = window_mixer(hc, hl, mix_p[0], mix_p[1], mix_p[2], need_ctx)
        else:
            ol, oc = hgrn2_mixer(hc, hl, mix_p[0], lb[0, l], lb[1, l], mix_p[1], mix_p[2], need_ctx)
        ffn = moe_swiglu if l % 2 == 1 else swiglu
        xl = xl + mod_l[2] * ol
        xl = xl + mod_l[5] * ffn(modulate(rmsnorm(xl, nf), mod_l[3], mod_l[4]), *ffn_p)
        if need_ctx:
            xc = xc + mod_c[2] * oc
            xc = xc + mod_c[5] * ffn(modulate(rmsnorm(xc, nf), mod_c[3], mod_c[4]), *ffn_p)
    return rmsnorm(xl, final_norm)
```

```python
import functools
import math

import jax
import jax.numpy as jnp
from jax import lax
from jax.experimental import pallas as pl
from jax.experimental.pallas import tpu as pltpu

F32 = jnp.float32
BF16 = jnp.bfloat16

HEAD_DIM = 128
GRID_W = 64
NA_WIN_ROWS = 8
NA_WIN_COLS = 16
SWA_WINDOW = 128
SWA_BLOCK = 128
HG_CHUNK = 64
ROPE_THETA = 10000.0
NORM_EPS = 1e-6
NEG_INF = -1e30
N_MIXERS = 3

V7X_VMEM_LIMIT_BYTES = 56 * 1024 * 1024


def _params(*semantics):
    return pltpu.CompilerParams(dimension_semantics=semantics,
                                vmem_limit_bytes=V7X_VMEM_LIMIT_BYTES)


def _largest_divisor(n, candidates):
    for c in candidates:
        if n % c == 0:
            return c
    raise ValueError(f"no tile in {candidates} divides {n}")


def _dot(a, b):
    return jnp.dot(a, b, preferred_element_type=F32)


def _dot_nt(a, b):
    return lax.dot_general(a, b, (((1,), (1,)), ((), ())), preferred_element_type=F32)


def _split_bf16(a, parts):
    out = []
    rem = a
    for _ in range(parts):
        p = rem.astype(BF16)
        out.append(p)
        rem = rem - p.astype(F32)
    return out


def _dot_f32(a, b):
    a1, a2 = _split_bf16(a, 2)
    b1, b2 = _split_bf16(b, 2)
    return _dot(a1, b1) + (_dot(a1, b2) + _dot(a2, b1))


def _silu(x):
    return x * jax.nn.sigmoid(x)


def _adaln_kernel(c_ref, down_ref, up_ref, bias_ref, o_ref):
    t = _dot_f32(_silu(c_ref[...]), down_ref[...])
    o_ref[...] = _dot_f32(t, up_ref[...]) + bias_ref[...]


def _adaln(cc, down, up, bias):
    d, r = down.shape
    n = up.shape[1]
    tn = _largest_divisor(n, (4096, 2048, 1024, 512, 256, 128))
    return pl.pallas_call(
        _adaln_kernel,
        grid=(n // tn,),
        in_specs=[pl.BlockSpec((8, d), lambda j: (0, 0)),
                  pl.BlockSpec((d, r), lambda j: (0, 0)),
                  pl.BlockSpec((r, tn), lambda j: (0, j)),
                  pl.BlockSpec((1, tn), lambda j: (0, j))],
        out_specs=pl.BlockSpec((8, tn), lambda j: (0, j)),
        out_shape=jax.ShapeDtypeStruct((8, n), F32),
        compiler_params=_params("arbitrary"),
        name="adaln",
    )(cc, down, up, bias.reshape(1, n))


def _norm_mod(x, g, shift, scale):
    xf = x.astype(F32)
    y = xf * lax.rsqrt(jnp.mean(xf * xf, axis=-1, keepdims=True) + NORM_EPS)
    return (y * g) * (1.0 + scale) + shift


def _norm_kernel(x_ref, g_ref, sh_ref, sc_ref, o_ref):
    o_ref[...] = _norm_mod(x_ref[...], g_ref[...], sh_ref[...], sc_ref[...]).astype(o_ref.dtype)


def _norm_router_kernel(x_ref, g_ref, sh_ref, sc_ref, r_ref, o_ref, c_ref):
    h = _norm_mod(x_ref[...], g_ref[...], sh_ref[...], sc_ref[...])
    o_ref[...] = h.astype(o_ref.dtype)
    logits = _dot_f32(h, r_ref[...])
    ne = logits.shape[-1]
    idx = lax.broadcasted_iota(jnp.int32, logits.shape, 1)
    m1 = jnp.max(logits, axis=-1, keepdims=True)
    i1 = jnp.min(jnp.where(logits == m1, idx, ne), axis=-1, keepdims=True)
    first = idx == i1
    rest = jnp.where(first, -jnp.inf, logits)
    m2 = jnp.max(rest, axis=-1, keepdims=True)
    i2 = jnp.min(jnp.where(rest == m2, idx, ne), axis=-1, keepdims=True)
    e = jnp.exp(m2 - m1)
    den = 1.0 + e
    c_ref[...] = jnp.where(first, 1.0 / den, 0.0) + jnp.where(idx == i2, e / den, 0.0)


def _norm(x, g, shift, scale, out_dtype, router=None):
    m, d = x.shape
    tm = _largest_divisor(m, (512, 256, 128, 64, 32, 16, 8))
    row = lambda a: a.reshape(1, d).astype(F32)
    vec = pl.BlockSpec((1, d), lambda i: (0, 0))
    blk = pl.BlockSpec((tm, d), lambda i: (i, 0))
    if router is None:
        return pl.pallas_call(
            _norm_kernel, grid=(m // tm,),
            in_specs=[blk, vec, vec, vec], out_specs=blk,
            out_shape=jax.ShapeDtypeStruct((m, d), out_dtype),
            compiler_params=_params("arbitrary"), name="norm_mod",
        )(x, row(g), row(shift), row(scale))
    ne = router.shape[1]
    return pl.pallas_call(
        _norm_router_kernel, grid=(m // tm,),
        in_specs=[blk, vec, vec, vec, pl.BlockSpec((d, ne), lambda i: (0, 0))],
        out_specs=[blk, pl.BlockSpec((tm, ne), lambda i: (i, 0))],
        out_shape=[jax.ShapeDtypeStruct((m, d), out_dtype),
                   jax.ShapeDtypeStruct((m, ne), F32)],
        compiler_params=_params("arbitrary"), name="norm_mod_router",
    )(x, row(g), row(shift), row(scale), router)


def _rotate_pairs(x, lane):
    return jnp.where((lane % 64) < 32, pltpu.roll(x, 96, 1), pltpu.roll(x, 32, 1))


def _mm_kernel(x_ref, w_ref, o_ref):
    o_ref[...] = _dot(x_ref[...], w_ref[...]).astype(o_ref.dtype)


def _mm_rope_kernel(x_ref, w_ref, cos_ref, sin_ref, o_ref, *, n_rope):
    acc = _dot(x_ref[...], w_ref[...])
    j = pl.program_id(1)

    @pl.when(j >= n_rope)
    def _():
        o_ref[...] = acc.astype(o_ref.dtype)

    @pl.when(j < n_rope)
    def _():
        cos = cos_ref[...]
        sin = sin_ref[...]
        lane = lax.broadcasted_iota(jnp.int32, cos.shape, 1)
        for hgrp in range(acc.shape[1] // HEAD_DIM):
            sl = slice(hgrp * HEAD_DIM, (hgrp + 1) * HEAD_DIM)
            blk = acc[:, sl]
            o_ref[:, sl] = (blk * cos + _rotate_pairs(blk, lane) * sin).astype(o_ref.dtype)


def _matmul(x, w, out_dtype, rope=None):
    m, k = x.shape
    n = w.shape[1]
    tm = _largest_divisor(m, (1024, 512, 256, 128))
    tn = _largest_divisor(n if rope is None else math.gcd(n, rope[2]), (1024, 512, 256, 128))
    xs = pl.BlockSpec((tm, k), lambda i, j: (i, 0))
    ws = pl.BlockSpec((k, tn), lambda i, j: (0, j))
    os_ = pl.BlockSpec((tm, tn), lambda i, j: (i, j))
    shape = jax.ShapeDtypeStruct((m, n), out_dtype)
    if rope is None:
        return pl.pallas_call(_mm_kernel, grid=(m // tm, n // tn), in_specs=[xs, ws], out_specs=os_,
                              out_shape=shape, compiler_params=_params("parallel", "arbitrary"),
                              name="matmul")(x, w)
    cos, sin, rope_cols = rope
    ts = pl.BlockSpec((tm, HEAD_DIM), lambda i, j: (i, 0))
    return pl.pallas_call(
        functools.partial(_mm_rope_kernel, n_rope=rope_cols // tn),
        grid=(m // tm, n // tn), in_specs=[xs, ws, ts, ts], out_specs=os_, out_shape=shape,
        compiler_params=_params("parallel", "arbitrary"), name="matmul_rope")(x, w, cos, sin)


def _swiglu_kernel(x_ref, wa_ref, wb_ref, o_ref):
    x = x_ref[...]
    a = _dot(x, wa_ref[...])
    b = _dot(x, wb_ref[...])
    o_ref[...] = (_silu(a) * b).astype(o_ref.dtype)


def _swiglu_expert_kernel(x_ref, wa_ref, wb_ref, c_ref, o_ref):
    x = x_ref[...]
    a = _dot(x, wa_ref[0])
    b = _dot(x, wb_ref[0])
    o_ref[...] = (c_ref[0] * (_silu(a) * b)).astype(o_ref.dtype)


def _swiglu_up(x, w_gu):
    m, k = x.shape
    f = w_gu.shape[1] // 2
    tm = _largest_divisor(m, (1024, 512, 256, 128))
    tn = _largest_divisor(f, (512, 256, 128))
    nb = f // tn
    return pl.pallas_call(
        _swiglu_kernel, grid=(m // tm, nb),
        in_specs=[pl.BlockSpec((tm, k), lambda i, j: (i, 0)),
                  pl.BlockSpec((k, tn), lambda i, j: (0, j)),
                  pl.BlockSpec((k, tn), lambda i, j: (0, nb + j))],
        out_specs=pl.BlockSpec((tm, tn), lambda i, j: (i, j)),
        out_shape=jax.ShapeDtypeStruct((m, f), BF16),
        compiler_params=_params("parallel", "arbitrary"), name="swiglu_up")(x, w_gu, w_gu)


def _swiglu_up_experts(x, w_gu, comb_t):
    m, k = x.shape
    ne = w_gu.shape[0]
    f = w_gu.shape[2] // 2
    tm = _largest_divisor(m, (1024, 512, 256, 128))
    tn = _largest_divisor(f, (512, 256, 128))
    nb = f // tn
    return pl.pallas_call(
        _swiglu_expert_kernel, grid=(m // tm, ne * nb),
        in_specs=[pl.BlockSpec((tm, k), lambda i, j: (i, 0)),
                  pl.BlockSpec((1, k, tn), lambda i, j: (j // nb, 0, j % nb)),
                  pl.BlockSpec((1, k, tn), lambda i, j: (j // nb, 0, nb + j % nb)),
                  pl.BlockSpec((1, tm, 1), lambda i, j: (j // nb, i, 0))],
        out_specs=pl.BlockSpec((tm, tn), lambda i, j: (i, j)),
        out_shape=jax.ShapeDtypeStruct((m, ne * f), BF16),
        compiler_params=_params("parallel", "arbitrary"), name="swiglu_up_experts")(x, w_gu, w_gu, comb_t)


def _mm_residual_kernel(x_ref, w_ref, r_ref, g_ref, o_ref, *scratch, nk):
    part = _dot(x_ref[...], w_ref[...])
    if nk == 1:
        o_ref[...] = r_ref[...] + g_ref[...] * part
        return
    acc_ref, = scratch
    kk = pl.program_id(2)

    @pl.when(kk == 0)
    def _():
        acc_ref[...] = part

    @pl.when((kk > 0) & (kk < nk - 1))
    def _():
        acc_ref[...] += part

    @pl.when(kk == nk - 1)
    def _():
        o_ref[...] = r_ref[...] + g_ref[...] * (acc_ref[...] + part)


def _pick_tk(k):
    if k <= 4096:
        return k
    for nk in range(2, 65):
        if k % nk == 0 and (k // nk) % 128 == 0 and k // nk <= 5632:
            return k // nk
    raise ValueError(f"no K tile for {k}")


def _matmul_residual(x, w, res, gate):
    m, k = x.shape
    n = w.shape[1]
    tk = _pick_tk(k)
    nk = k // tk
    tm = _largest_divisor(m, (512, 256, 128))
    tn = _largest_divisor(n, (1024, 512, 256, 128))
    scratch = [] if nk == 1 else [pltpu.VMEM((tm, tn), F32)]
    return pl.pallas_call(
        functools.partial(_mm_residual_kernel, nk=nk),
        grid=(m // tm, n // tn, nk),
        in_specs=[pl.BlockSpec((tm, tk), lambda i, j, kk: (i, kk)),
                  pl.BlockSpec((tk, tn), lambda i, j, kk: (kk, j)),
                  pl.BlockSpec((tm, tn), lambda i, j, kk: (i, j)),
                  pl.BlockSpec((1, tn), lambda i, j, kk: (0, j))],
        out_specs=pl.BlockSpec((tm, tn), lambda i, j, kk: (i, j)),
        out_shape=jax.ShapeDtypeStruct((m, n), F32),
        scratch_shapes=scratch,
        compiler_params=_params("parallel", "arbitrary", "arbitrary"),
        name="matmul_residual")(x, w, res, gate.reshape(1, n).astype(F32))


def _softmax_pv(s_parts, v_parts, extra_logit=None):
    m = functools.reduce(jnp.maximum, [jnp.max(s, axis=-1, keepdims=True) for s in s_parts])
    if extra_logit is not None:
        m = jnp.maximum(m, extra_logit)
    ps = [jnp.exp(s - m) for s in s_parts]
    den = functools.reduce(lambda a, b: a + b, [jnp.sum(p, axis=-1, keepdims=True) for p in ps])
    if extra_logit is not None:
        den = den + jnp.exp(extra_logit - m)
    o = functools.reduce(lambda a, b: a + b, [_dot(p.astype(BF16), v) for p, v in zip(ps, v_parts)])
    return o / den


def _na_kernel(q_ref, k_ref, v_ref, kc_ref, vc_ref, b_ref, o_ref, *, rows_per_step, rows, scale):
    rg = pl.program_id(1)
    kc = kc_ref[...]
    vc = vc_ref[...]
    band = NA_WIN_ROWS * GRID_W

    def one_row(i, carry):
        r = rg * rows_per_step + i
        rs = jnp.clip(r - NA_WIN_ROWS // 2, 0, rows - NA_WIN_ROWS)
        start = pl.multiple_of(rs * GRID_W, GRID_W)
        qsl = pl.ds(pl.multiple_of(i * GRID_W, GRID_W), GRID_W)
        q = q_ref[qsl, :]
        kb = k_ref[pl.ds(start, band), :]
        vb = v_ref[pl.ds(start, band), :]
        s_loc = _dot_nt(q, kb) * scale + b_ref[0, rs - r + (NA_WIN_ROWS - 1)]
        s_ctx = _dot_nt(q, kc) * scale
        o_ref[qsl, :] = _softmax_pv([s_loc, s_ctx], [vb, vc]).astype(o_ref.dtype)
        return carry

    lax.fori_loop(0, rows_per_step, one_row, 0)


def _na_bias_table(rpb):
    q = jnp.arange(GRID_W)[:, None]
    c = jnp.arange(GRID_W)[None, :]
    cs = jnp.clip(q - NA_WIN_COLS // 2, 0, GRID_W - NA_WIN_COLS)
    in_win = (c >= cs) & (c < cs + NA_WIN_COLS)
    col_off = jnp.clip(c - q + (NA_WIN_COLS - 1), 0, 2 * NA_WIN_COLS - 2)
    t = jnp.where(in_win, rpb.astype(F32)[:, :, col_off], NEG_INF)
    row_off = jnp.arange(NA_WIN_ROWS)[:, None] + jnp.arange(NA_WIN_ROWS)[None, :]
    tb = t[:, row_off]
    tb = jnp.transpose(tb, (0, 1, 3, 2, 4))
    return tb.reshape(rpb.shape[0], NA_WIN_ROWS, GRID_W, NA_WIN_ROWS * GRID_W)


def _na_attention(z, zc, rpb, d):
    t = z.shape[0]
    lc = zc.shape[0]
    nh = d // HEAD_DIM
    rows = t // GRID_W
    assert rows >= NA_WIN_ROWS and t % GRID_W == 0
    rps = _largest_divisor(rows, (8, 4, 2, 1))
    bias = _na_bias_table(rpb)
    tq = rps * GRID_W
    return pl.pallas_call(
        functools.partial(_na_kernel, rows_per_step=rps, rows=rows, scale=HEAD_DIM ** -0.5),
        grid=(nh, rows // rps),
        in_specs=[pl.BlockSpec((tq, HEAD_DIM), lambda h, g: (g, h)),
                  pl.BlockSpec((t, HEAD_DIM), lambda h, g: (0, nh + h)),
                  pl.BlockSpec((t, HEAD_DIM), lambda h, g: (0, 2 * nh + h)),
                  pl.BlockSpec((lc, HEAD_DIM), lambda h, g: (0, nh + h)),
                  pl.BlockSpec((lc, HEAD_DIM), lambda h, g: (0, 2 * nh + h)),
                  pl.BlockSpec((1, NA_WIN_ROWS, GRID_W, NA_WIN_ROWS * GRID_W), lambda h, g: (h, 0, 0, 0))],
        out_specs=pl.BlockSpec((tq, HEAD_DIM), lambda h, g: (g, h)),
        out_shape=jax.ShapeDtypeStruct((t, d), BF16),
        compiler_params=_params("parallel", "arbitrary"), name="na_attention")(z, z, z, zc, zc, bias)


def _swa_kernel(sink_ref, q_ref, k_ref, v_ref, kc_ref, vc_ref, o_ref, *, group, t, scale):
    kv = pl.program_id(0)
    b = pl.program_id(1)
    span = SWA_BLOCK + 2 * SWA_WINDOW
    start = pl.multiple_of(jnp.clip(b * SWA_BLOCK - SWA_WINDOW, 0, t - span), SWA_BLOCK)
    kw = k_ref[pl.ds(start, span), :]
    vw = v_ref[pl.ds(start, span), :]
    kc = kc_ref[...]
    vc = vc_ref[...]
    qpos = b * SWA_BLOCK + lax.broadcasted_iota(jnp.int32, (SWA_BLOCK, span), 0)
    kpos = start + lax.broadcasted_iota(jnp.int32, (SWA_BLOCK, span), 1)
    valid = jnp.abs(qpos - kpos) <= SWA_WINDOW
    for j in range(group):
        sl = slice(j * HEAD_DIM, (j + 1) * HEAD_DIM)
        q = q_ref[:, sl]
        s_loc = jnp.where(valid, _dot_nt(q, kw) * scale, NEG_INF)
        s_ctx = _dot_nt(q, kc) * scale
        o_ref[:, sl] = _softmax_pv([s_loc, s_ctx], [vw, vc], sink_ref[kv, j]).astype(o_ref.dtype)


def _swa_attention(z, zc, sinks, d, kvh):
    t = z.shape[0]
    lc = zc.shape[0]
    nh = d // HEAD_DIM
    group = nh // kvh
    span = SWA_BLOCK + 2 * SWA_WINDOW
    assert t % SWA_BLOCK == 0 and t >= span
    gw = group * HEAD_DIM
    return pl.pallas_call(
        functools.partial(_swa_kernel, group=group, t=t, scale=HEAD_DIM ** -0.5),
        grid=(kvh, t // SWA_BLOCK),
        in_specs=[pl.BlockSpec(memory_space=pltpu.SMEM),
                  pl.BlockSpec((SWA_BLOCK, gw), lambda kv, b: (b, kv)),
                  pl.BlockSpec((t, HEAD_DIM), lambda kv, b: (0, nh + kv)),
                  pl.BlockSpec((t, HEAD_DIM), lambda kv, b: (0, nh + kvh + kv)),
                  pl.BlockSpec((lc, HEAD_DIM), lambda kv, b: (0, nh + kv)),
                  pl.BlockSpec((lc, HEAD_DIM), lambda kv, b: (0, nh + kvh + kv))],
        out_specs=pl.BlockSpec((SWA_BLOCK, gw), lambda kv, b: (b, kv)),
        out_shape=jax.ShapeDtypeStruct((t, d), BF16),
        compiler_params=_params("parallel", "arbitrary"), name="swa_attention",
    )(sinks.astype(F32).reshape(kvh, group), z, z, z, zc, zc)


def _ctx_attn_kernel(sink_ref, q_ref, k_ref, v_ref, o_ref, *, group, use_sink, scale):
    kv = pl.program_id(0)
    k = k_ref[...]
    v = v_ref[...]
    for j in range(group):
        sl = slice(j * HEAD_DIM, (j + 1) * HEAD_DIM)
        s = _dot_nt(q_ref[:, sl], k) * scale
        extra = sink_ref[kv, j] if use_sink else None
        o_ref[:, sl] = _softmax_pv([s], [v], extra).astype(o_ref.dtype)


def _ctx_attention(zc, d, kvh, sinks=None):
    lc = zc.shape[0]
    nh = d // HEAD_DIM
    group = nh // kvh
    gw = group * HEAD_DIM
    use_sink = sinks is not None
    sink_arr = (sinks if use_sink else jnp.zeros((nh,), F32)).astype(F32).reshape(kvh, group)
    return pl.pallas_call(
        functools.partial(_ctx_attn_kernel, group=group, use_sink=use_sink, scale=HEAD_DIM ** -0.5),
        grid=(kvh,),
        in_specs=[pl.BlockSpec(memory_space=pltpu.SMEM),
                  pl.BlockSpec((lc, gw), lambda kv: (0, kv)),
                  pl.BlockSpec((lc, HEAD_DIM), lambda kv: (0, nh + kv)),
                  pl.BlockSpec((lc, HEAD_DIM), lambda kv: (0, nh + kvh + kv))],
        out_specs=pl.BlockSpec((lc, gw), lambda kv: (0, kv)),
        out_shape=jax.ShapeDtypeStruct((lc, d), BF16),
        compiler_params=_params("arbitrary"), name="ctx_attention")(sink_arr, zc, zc, zc)


def _hgrn_kernel(*refs, block, reverse, readout):
    if readout:
        q_ref, i_ref, f_ref, lb_ref, s0_ref, other_ref, gate_ref, gn_ref, o_ref, sfin_ref, st_ref = refs
    else:
        q_ref, i_ref, f_ref, lb_ref, s0_ref, o_ref, sfin_ref, st_ref = refs
    j = pl.program_id(1)
    c = HG_CHUNK

    @pl.when(j == 0)
    def _():
        st_ref[...] = s0_ref[0]

    lb = lb_ref[...]
    row = lax.broadcasted_iota(jnp.int32, (c, c), 0)
    col = lax.broadcasted_iota(jnp.int32, (c, c), 1)
    keep = (col >= row) if reverse else (col <= row)
    tri = keep.astype(BF16)
    edge = 0 if reverse else c - 1
    nc = block // c
    order = range(nc - 1, -1, -1) if reverse else range(nc)
    for ci in order:
        sl = slice(ci * c, (ci + 1) * c)
        q = _silu(q_ref[sl, :])
        v = i_ref[sl, :]
        f = lb + (1.0 - lb) * jax.nn.sigmoid(f_ref[sl, :])
        k = 1.0 - f
        g = jnp.log(f)
        g1, g2, g3 = _split_bf16(g, 3)
        b = _dot(tri, g1) + (_dot(tri, g2) + _dot(tri, g3))
        b_tot = b[edge:edge + 1, :]
        qe = (q * jnp.exp(b)).astype(BF16)
        ke = (k * jnp.exp(-b)).astype(BF16)
        kd = (k * jnp.exp(b_tot - b)).astype(BF16)
        vb = v.astype(BF16)
        a = jnp.where(keep, _dot_nt(qe, ke), 0.0)
        st = st_ref[...]
        o = _dot_nt(qe, st.astype(BF16)) + _dot(a.astype(BF16), vb)
        st_ref[...] = st * jnp.exp(b_tot) + _dot(v.T.astype(BF16), kd)
        if readout:
            tot = o + other_ref[sl, :]
            y = tot * lax.rsqrt(jnp.mean(tot * tot, axis=-1, keepdims=True) + NORM_EPS)
            o_ref[sl, :] = ((y * gn_ref[...]) * _silu(gate_ref[sl, :])).astype(o_ref.dtype)
        else:
            o_ref[sl, :] = o

    @pl.when(j == pl.num_programs(1) - 1)
    def _():
        sfin_ref[0] = st_ref[...]


def _hgrn_scan(z, lb, s0, d, f_col, reverse, readout=None):
    t = z.shape[0]
    nh = d // HEAD_DIM
    block = _largest_divisor(t, (512, 256, 128, 64))
    nb = t // block
    pos = (lambda j: nb - 1 - j) if reverse else (lambda j: j)
    col = lambda off: pl.BlockSpec((block, HEAD_DIM), lambda h, j: (pos(j), off * nh + h))
    in_specs = [col(0), col(1), col(f_col),
                pl.BlockSpec((1, HEAD_DIM), lambda h, j: (0, h)),
                pl.BlockSpec((1, HEAD_DIM, HEAD_DIM), lambda h, j: (h, 0, 0))]
    args = [z, z, z, lb.reshape(1, d).astype(F32), s0]
    if readout is not None:
        other, g_norm = readout
        in_specs += [col(0), col(2), pl.BlockSpec((1, HEAD_DIM), lambda h, j: (0, 0))]
        args += [other, z, g_norm.reshape(1, HEAD_DIM).astype(F32)]
    out_dtype = BF16 if readout is not None else F32
    return pl.pallas_call(
        functools.partial(_hgrn_kernel, block=block, reverse=reverse, readout=readout is not None),
        grid=(nh, nb),
        in_specs=in_specs,
        out_specs=[pl.BlockSpec((block, HEAD_DIM), lambda h, j: (pos(j), h)),
                   pl.BlockSpec((1, HEAD_DIM, HEAD_DIM), lambda h, j: (h, 0, 0))],
        out_shape=[jax.ShapeDtypeStruct((t, d), out_dtype),
                   jax.ShapeDtypeStruct((nh, HEAD_DIM, HEAD_DIM), F32)],
        scratch_shapes=[pltpu.VMEM((HEAD_DIM, HEAD_DIM), F32)],
        compiler_params=_params("parallel", "arbitrary"), name="hgrn_scan")(*args)


def _hgrn_bidir(z, lb_f, lb_b, s_f, s_b, g_norm, d):
    o_b, sb = _hgrn_scan(z, lb_b, s_b, d, 4, True)
    y, sf = _hgrn_scan(z, lb_f, s_f, d, 3, False, readout=(o_b, g_norm))
    return y, sf, sb


def _rope_tables(t):
    pos = jnp.arange(t)
    quarter = HEAD_DIM // 4
    inv = ROPE_THETA ** (-jnp.arange(quarter, dtype=F32) / quarter)
    ang_r = (pos // GRID_W).astype(F32)[:, None] * inv[None, :]
    ang_c = (pos % GRID_W).astype(F32)[:, None] * inv[None, :]
    cos = jnp.concatenate([jnp.cos(ang_r)] * 2 + [jnp.cos(ang_c)] * 2, axis=-1)
    sin = jnp.concatenate([-jnp.sin(ang_r), jnp.sin(ang_r), -jnp.sin(ang_c), jnp.sin(ang_c)], axis=-1)
    return cos.astype(F32), sin.astype(F32)


def _ffn_dense(h, w_gu, w_down, res, gate):
    act = _swiglu_up(h, w_gu.astype(BF16))
    return _matmul_residual(act, w_down.astype(BF16), res, gate)


def _ffn_moe(h, comb, w_gu, w_down, res, gate):
    ne, f, d = w_down.shape
    comb_t = jnp.transpose(comb)[:, :, None]
    act = _swiglu_up_experts(h, w_gu.astype(BF16), comb_t)
    return _matmul_residual(act, w_down.astype(BF16).reshape(ne * f, d), res, gate)


def kernel(x, c, ctx, c_ctx, norm_mix_0, norm_ffn_0, ada_down_0, ada_up_0, ada_bias_0, na_w_in_0, na_rpb_0, na_w_out_0, ffn_w_gu_0, ffn_w_down_0, norm_mix_1, norm_ffn_1, ada_down_1, ada_up_1, ada_bias_1, swa_w_in_1, swa_sinks_1, swa_w_out_1, moe_router_1, moe_w_gu_1, moe_w_down_1, norm_mix_2, norm_ffn_2, ada_down_2, ada_up_2, ada_bias_2, hg_w_in_2, hg_gnorm_2, hg_w_out_2, ffn_w_gu_2, ffn_w_down_2, norm_mix_3, norm_ffn_3, ada_down_3, ada_up_3, ada_bias_3, na_w_in_3, na_rpb_3, na_w_out_3, moe_router_3, moe_w_gu_3, moe_w_down_3, hgrn_lower_bounds, final_norm):
    layers = [
        ((norm_mix_0, norm_ffn_0, ada_down_0, ada_up_0, ada_bias_0),
         (na_w_in_0, na_rpb_0, na_w_out_0), (ffn_w_gu_0, ffn_w_down_0)),
        ((norm_mix_1, norm_ffn_1, ada_down_1, ada_up_1, ada_bias_1),
         (swa_w_in_1, swa_sinks_1, swa_w_out_1), (moe_router_1, moe_w_gu_1, moe_w_down_1)),
        ((norm_mix_2, norm_ffn_2, ada_down_2, ada_up_2, ada_bias_2),
         (hg_w_in_2, hg_gnorm_2, hg_w_out_2), (ffn_w_gu_2, ffn_w_down_2)),
        ((norm_mix_3, norm_ffn_3, ada_down_3, ada_up_3, ada_bias_3),
         (na_w_in_3, na_rpb_3, na_w_out_3), (moe_router_3, moe_w_gu_3, moe_w_down_3)),
    ]
    depth = len(layers)
    batch, t, d = x.shape
    assert batch == 1 and c.shape[0] == 1 and ctx.shape[0] == 1
    lb = jax.nn.softmax(hgrn_lower_bounds.astype(F32), axis=1)
    lb = jnp.cumsum(lb, axis=1) - lb[:, :1]

    xl = x[0].astype(F32)
    xc = ctx[0].astype(F32)
    cc = jnp.zeros((8, d), F32).at[0].set(c[0]).at[1].set(c_ctx)
    for l in range(depth):
        (nm, nf, ad, au, ab), mix_p, ffn_p = layers[l]
        need_ctx = l < depth - 1
        mod = _adaln(cc, ad.astype(F32), au.astype(F32), ab.astype(F32))
        mod_l = jnp.split(mod[0], 6)
        mod_c = jnp.split(mod[1], 6)
        hl = _norm(xl, nm, mod_l[0], mod_l[1], BF16)
        hc = _norm(xc, nm, mod_c[0], mod_c[1], BF16)
        kind = l % N_MIXERS
        w_in = mix_p[0].astype(BF16)
        w_out = mix_p[-1].astype(BF16)
        oc = None
        if kind == 0:
            zl = _matmul(hl, w_in, BF16)
            zc = _matmul(hc, w_in, BF16)
            ol = _na_attention(zl, zc, mix_p[1], d)
            if need_ctx:
                oc = _ctx_attention(zc, d, d // HEAD_DIM)
        elif kind == 1:
            kvh = (w_in.shape[1] - d) // (2 * HEAD_DIM)
            cos, sin = _rope_tables(t)
            zl = _matmul(hl, w_in, BF16, rope=(cos, sin, d + kvh * HEAD_DIM))
            zc = _matmul(hc, w_in, BF16)
            ol = _swa_attention(zl, zc, mix_p[1], d, kvh)
            if need_ctx:
                oc = _ctx_attention(zc, d, kvh, sinks=mix_p[1])
        else:
            zl = _matmul(hl, w_in, F32)
            zc = _matmul(hc, w_in, F32)
            zeros = jnp.zeros((d // HEAD_DIM, HEAD_DIM, HEAD_DIM), F32)
            oc, sc_f, sc_b = _hgrn_bidir(zc, lb[0, l], lb[1, l], zeros, zeros, mix_p[1], d)
            ol, _, _ = _hgrn_bidir(zl, lb[0, l], lb[1, l], sc_f, sc_b, mix_p[1], d)
        xl = _matmul_residual(ol, w_out, xl, mod_l[2])
        if need_ctx:
            xc = _matmul_residual(oc, w_out, xc, mod_c[2])
        if l % 2 == 1:
            router, w_gu, w_down = ffn_p
            hl, comb_l = _norm(xl, nf, mod_l[3], mod_l[4], BF16, router=router.astype(F32))
            xl = _ffn_moe(hl, comb_l, w_gu, w_down, xl, mod_l[5])
            if need_ctx:
                hc, comb_c = _norm(xc, nf, mod_c[3], mod_c[4], BF16, router=router.astype(F32))
                xc = _ffn_moe(hc, comb_c, w_gu, w_down, xc, mod_c[5])
        else:
            w_gu, w_down = ffn_p
            hl = _norm(xl, nf, mod_l[3], mod_l[4], BF16)
            xl = _ffn_dense(hl, w_gu, w_down, xl, mod_l[5])
            if need_ctx:
                hc = _norm(xc, nf, mod_c[3], mod_c[4], BF16)
                xc = _ffn_dense(hc, w_gu, w_down, xc, mod_c[5])
    zero = jnp.zeros((d,), F32)
    return _norm(xl, final_norm, zero, zero, x.dtype)[None]
```
